```python
import math
import jax
import jax.numpy as jnp
from jax import lax
import numpy as np

D_MODEL = 1024
BATCH = 1
SEQ = 16384
DEPTH = 2
DEC_BATCH = 32
DEC_SEQ = 4
PAST_LEN = 16384
PAGE_SIZE = 128

F32 = jnp.float32
N_A_LAYERS = DEPTH // 2
N_B_LAYERS = DEPTH - N_A_LAYERS

SSM_D_INNER = 2 * D_MODEL
SSM_HEAD_DIM = 64
SSM_HEADS = SSM_D_INNER // SSM_HEAD_DIM
SSM_GROUPS = 4
SSM_HPG = SSM_HEADS // SSM_GROUPS
SSM_STATE = 128
SSM_CONV = 4
SSM_CONV_DIM = SSM_D_INNER + 2 * SSM_GROUPS * SSM_STATE
SSM_CHUNK = 128
DT_MIN = 1e-3
DT_MAX = 1e-1
RMS_EPS = 1e-5

DIL_WINDOWS = (128, 512, 2048)
DIL_RATES = (1, 4, 16)
DIL_GROUPS = 3
DIL_SLOTS = 4
DIL_HEAD_DIM = 128
DIL_STEPS = DIL_WINDOWS[0] // DIL_RATES[0]
DIL_Q_DIM = DIL_GROUPS * DIL_SLOTS * DIL_HEAD_DIM
DIL_OUT_DIM = DIL_SLOTS * DIL_HEAD_DIM
DIL_KV_DIM = DIL_GROUPS * 2 * DIL_SLOTS * DIL_HEAD_DIM
DIL_SCALE = DIL_HEAD_DIM ** -0.5

MEM_TOKENS = 256
MEM_HEADS = 4
MEM_HEAD_DIM = 256
MEM_DIM = MEM_HEADS * MEM_HEAD_DIM
MEM_SCALE = MEM_HEAD_DIM ** -0.5

REL_BUCKETS = 32
REL_MAX_DIST = 2048

N_EXPERTS = 32
TOP_K = 4
D_EXPERT = D_MODEL
SWIGLU_LIMIT = 7.0
SWIGLU_ALPHA = 1.702
MOE_BLOCK = 128

A_MIX_END = SSM_D_INNER + SSM_CONV_DIM + SSM_HEADS
A_IN_DIM = A_MIX_END + MEM_DIM
A_OUT_IN = SSM_D_INNER + MEM_DIM
B_IN_DIM = DIL_Q_DIM + MEM_DIM
B_OUT_IN = DIL_OUT_DIM + MEM_DIM

DN_ALPHA = (2 * DEPTH) ** 0.25
DN_BETA = (8 * DEPTH) ** -0.25
LN_EPS = 1e-5

kernel_name = 'yoco_ssd_dilated_moe_step'


def layer_norm(x, g, b):
    xf = x.astype(F32)
    xc = xf - jnp.mean(xf, -1, keepdims=True)
    var = jnp.mean(xc * xc, -1, keepdims=True)
    return (xc * lax.rsqrt(var + LN_EPS)).astype(x.dtype) * g + b


def gated_rmsnorm(y, z, w):
    hf = (y * jax.nn.silu(z)).astype(F32)
    hg = hf.reshape(hf.shape[:-1] + (SSM_GROUPS, SSM_D_INNER // SSM_GROUPS))
    hg = hg * lax.rsqrt(jnp.mean(hg * hg, -1, keepdims=True) + RMS_EPS)
    return hg.reshape(hf.shape).astype(y.dtype) * w


def causal_conv(xbc, buf, w, b):
    L = xbc.shape[1]
    xp = jnp.concatenate([buf.astype(xbc.dtype), xbc], axis=1)
    y = b
    for k in range(SSM_CONV):
        y = y + xp[:, k:k + L] * w[k]
    return jax.nn.silu(y), xp[:, L:]


def ssd_scan(x, dt, a, bm, cm, s0):
    b, L = x.shape[:2]
    q = math.gcd(L, SSM_CHUNK)
    c = L // q
    G, E, P, N = SSM_GROUPS, SSM_HPG, SSM_HEAD_DIM, SSM_STATE
    xdt = (x.astype(F32) * dt[..., None]).reshape(b, c, q, G, E, P)
    da = (dt * a).reshape(b, c, q, G, E)
    bc = bm.astype(F32).reshape(b, c, q, G, N)
    cc = cm.astype(F32).reshape(b, c, q, G, N)
    a_cs = jnp.cumsum(da, axis=2)
    seg = a_cs[:, :, :, None] - a_cs[:, :, None, :]
    causal = jnp.tril(jnp.ones((q, q), bool))[:, :, None, None]
    decay = jnp.exp(jnp.where(causal, seg, -jnp.inf))
    cb = jnp.einsum('bclgn,bcsgn->bclsg', cc, bc)
    y_diag = jnp.einsum('bclsge,bcsgep->bclgep', cb[..., None] * decay, xdt)
    decay_end = jnp.exp(a_cs[:, :, -1:] - a_cs)
    s_chunk = jnp.einsum('bclgn,bclgep->bcgepn', bc, xdt * decay_end[..., None])
    chunk_decay = jnp.exp(a_cs[:, :, -1])

    def carry(s, inp):
        sc, dc = inp
        return dc[..., None, None] * s + sc, s

    s_last, s_prev = lax.scan(carry, s0.astype(F32).reshape(b, G, E, P, N),
                              (jnp.moveaxis(s_chunk, 1, 0), jnp.moveaxis(chunk_decay, 1, 0)))
    s_prev = jnp.moveaxis(s_prev, 0, 1)
    y_off = jnp.einsum('bclgn,bcgepn->bclgep', cc, s_prev) * jnp.exp(a_cs)[..., None]
    y = (y_diag + y_off).reshape(b, L, SSM_HEADS, P)
    return y.astype(x.dtype), s_last.reshape(b, SSM_HEADS, P, N).astype(s0.dtype)


def t5_bucket(dist):
    max_exact = REL_BUCKETS // 2
    d_f = jnp.maximum(dist, 1).astype(F32)
    large = max_exact + (jnp.log(d_f / max_exact) / math.log(REL_MAX_DIST / max_exact)
                         * (REL_BUCKETS - max_exact)).astype(jnp.int32)
    large = jnp.minimum(large, REL_BUCKETS - 1)
    return jnp.where(dist < max_exact, dist, large)


def dilated_prompt(q, k, v, bias, rate):
    b, T, H, Dh = q.shape
    S = DIL_STEPS
    span = S * rate
    t_pad = -(-T // span) * span
    nb = t_pad // span

    def blocks(t):
        t = jnp.pad(t, ((0, 0), (0, t_pad - T), (0, 0), (0, 0)))
        return t.reshape(b, nb, S, rate, H, Dh)

    def with_prev(t):
        prev = jnp.concatenate([jnp.zeros_like(t[:, :1]), t[:, :-1]], axis=1)
        return jnp.concatenate([prev, t], axis=2)

    qb = blocks(q)
    kk = with_prev(blocks(k))
    vv = with_prev(blocks(v))
    step = S + jnp.arange(S)[:, None] - jnp.arange(2 * S)[None, :]
    band = (step >= 0) & (step <= S)
    has_prev = (jnp.arange(nb) > 0)[:, None, None] | (jnp.arange(2 * S) >= S)[None, None, :]
    valid = band[None] & has_prev
    bias_qk = jnp.moveaxis(bias[jnp.clip(step, 0, S)], -1, 0).astype(F32)
    s = jnp.einsum('bnqrhd,bnkrhd->bnrhqk', qb, kk).astype(F32) * DIL_SCALE + bias_qk
    s = jnp.where(valid[None, :, None, None], s, -jnp.inf)
    m = jnp.max(s, -1, keepdims=True)
    e = jnp.exp(s - m)
    den = jnp.sum(e, -1, keepdims=True)
    o = jnp.einsum('bnrhqk,bnkrhd->bnqrhd', (e / den).astype(v.dtype), vv)
    lse = jnp.transpose((m + jnp.log(den))[..., 0], (0, 1, 4, 2, 3))
    return o.reshape(b, t_pad, H, Dh)[:, :T], lse.reshape(b, t_pad, H)[:, :T]


def dilated_sample(q, kv_all, bias, rate):
    b, Ls = q.shape[:2]
    Lb = kv_all.shape[1] - Ls
    S = DIL_STEPS
    idx = Lb + jnp.arange(Ls)[:, None] - rate * jnp.arange(S + 1)[None, :]
    valid = idx >= 0
    g = kv_all[:, jnp.maximum(idx, 0)]
    s = jnp.einsum('bjhd,bjshd->bhjs', q, g[:, :, :, 0]).astype(F32) * DIL_SCALE
    s = s + bias.T.astype(F32)[:, None, :]
    s = jnp.where(valid, s, -jnp.inf)
    m = jnp.max(s, -1, keepdims=True)
    e = jnp.exp(s - m)
    den = jnp.sum(e, -1, keepdims=True)
    o = jnp.einsum('bhjs,bjshd->bjhd', (e / den).astype(q.dtype), g[:, :, :, 1])
    lse = jnp.transpose((m + jnp.log(den))[..., 0], (0, 2, 1))
    return o, lse


def combine_dilations(outs, lses):
    wgt = jax.nn.softmax(jnp.stack(lses, 2), axis=2)
    o = jnp.einsum('blgh,blghd->blhd', wgt.astype(outs[0].dtype), jnp.stack(outs, 2))
    return o.reshape(o.shape[0], o.shape[1], DIL_OUT_DIM)


def mem_attention(q, mem_kv):
    s = jnp.einsum('blhd,bmhd->bhlm', q, mem_kv[:, :, 0]).astype(F32) * MEM_SCALE
    p = jax.nn.softmax(s, -1).astype(q.dtype)
    o = jnp.einsum('bhlm,bmhd->blhd', p, mem_kv[:, :, 1])
    return o.reshape(q.shape[0], q.shape[1], MEM_DIM)


def moe(h, router_w, router_b, w_gu, b_gu, w_down, b_down):
    bsz, L, D = h.shape
    x = h.reshape(-1, D)
    T = x.shape[0]
    logits = (x @ router_w + router_b).astype(F32)
    top_v, top_i = lax.top_k(logits, TOP_K)
    gates = jax.nn.softmax(top_v, -1)
    A = T * TOP_K
    flat_e = top_i.reshape(-1)
    order = jnp.argsort(flat_e)
    sorted_e = flat_e[order]
    counts = jnp.bincount(flat_e, length=N_EXPERTS)
    padded = (counts + MOE_BLOCK - 1) // MOE_BLOCK * MOE_BLOCK
    pad_end = jnp.cumsum(padded)
    pad_start = pad_end - padded
    start = jnp.cumsum(counts) - counts
    dest = pad_start[sorted_e] + jnp.arange(A) - start[sorted_e]
    n_blocks = -(-A // MOE_BLOCK) + N_EXPERTS
    slot_tok = jnp.full((n_blocks * MOE_BLOCK,), T, jnp.int32).at[dest].set((order // TOP_K).astype(jnp.int32))
    slot_gate = jnp.zeros((n_blocks * MOE_BLOCK,), F32).at[dest].set(gates.reshape(-1)[order])
    block_e = jnp.minimum(jnp.searchsorted(pad_end, jnp.arange(n_blocks) * MOE_BLOCK, side='right'),
                          N_EXPERTS - 1)
    xpad = jnp.concatenate([x, jnp.zeros((1, D), x.dtype)], axis=0)
    xs = xpad[slot_tok].reshape(n_blocks, MOE_BLOCK, D)

    def expert_block(args):
        xb, e = args
        gu = xb @ w_gu[e] + b_gu[e]
        gl = jnp.minimum(gu[..., :D_EXPERT], SWIGLU_LIMIT)
        up = jnp.clip(gu[..., D_EXPERT:], -SWIGLU_LIMIT, SWIGLU_LIMIT)
        act = gl * jax.nn.sigmoid(SWIGLU_ALPHA * gl) * (up + 1.0)
        return act @ w_down[e] + b_down[e]

    ys = lax.map(expert_block, (xs, block_e)).reshape(-1, D)
    out = jnp.zeros((T + 1, D), ys.dtype).at[slot_tok].add(ys * slot_gate[:, None].astype(ys.dtype))[:T]
    return out.reshape(bsz, L, D)


def trunk(x, mem_kv, conv_state, ssm_state, win_caches, w):
    b, L, _ = x.shape
    h = x
    new_conv, new_ssm, new_win, kv_rows, biases = [], [], [], [], []
    for layer in range(DEPTH):
        if layer < N_A_LAYERS:
            i = layer
            proj = h @ w['a_w_in'][i]
            z = proj[..., :SSM_D_INNER]
            xbc = proj[..., SSM_D_INNER:SSM_D_INNER + SSM_CONV_DIM]
            dt_raw = proj[..., SSM_D_INNER + SSM_CONV_DIM:A_MIX_END]
            mq = proj[..., A_MIX_END:]
            xbc, conv_new = causal_conv(xbc, conv_state[i], w['a_conv_w'][i], w['a_conv_b'][i])
            xs = xbc[..., :SSM_D_INNER].reshape(b, L, SSM_HEADS, SSM_HEAD_DIM)
            bm = xbc[..., SSM_D_INNER:SSM_D_INNER + SSM_GROUPS * SSM_STATE].reshape(b, L, SSM_GROUPS, SSM_STATE)
            cm = xbc[..., SSM_D_INNER + SSM_GROUPS * SSM_STATE:].reshape(b, L, SSM_GROUPS, SSM_STATE)
            dt = jax.nn.softplus(dt_raw.astype(F32) + w['a_dt_bias'][i].astype(F32))
            a = -jnp.exp(w['a_a_log'][i].astype(F32))
            y, s_new = ssd_scan(xs, dt, a, bm, cm, ssm_state[i])
            y = y + xs * w['a_d_skip'][i][:, None]
            y = gated_rmsnorm(y.reshape(b, L, SSM_D_INNER), z, w['a_norm_w'][i])
            mo = mem_attention(mq.reshape(b, L, MEM_HEADS, MEM_HEAD_DIM), mem_kv[layer])
            mix = jnp.concatenate([y, mo], -1) @ w['a_w_out'][i]
            new_conv.append(conv_new)
            new_ssm.append(s_new)
        else:
            if layer == N_A_LAYERS:
                kv = (h @ w['w_kv_shared']).reshape(b, L, DIL_GROUPS, 2, DIL_SLOTS, DIL_HEAD_DIM)
                for g in range(DIL_GROUPS):
                    kv_g = kv[:, :, g]
                    if win_caches is not None:
                        kv_g = jnp.concatenate([win_caches[g].astype(kv_g.dtype), kv_g], axis=1)
                    kv_rows.append(kv_g)
                    new_win.append(kv_g[:, -min(DIL_WINDOWS[g], kv_g.shape[1]):])
                    bucket = t5_bucket(DIL_RATES[g] * jnp.arange(DIL_STEPS + 1))
                    biases.append(w['rel_bias'][bucket][:, g * DIL_SLOTS:(g + 1) * DIL_SLOTS])
            j = layer - N_A_LAYERS
            proj = h @ w['b_w_in'][j]
            q = proj[..., :DIL_Q_DIM].reshape(b, L, DIL_GROUPS, DIL_SLOTS, DIL_HEAD_DIM)
            mq = proj[..., DIL_Q_DIM:]
            outs, lses = [], []
            for g in range(DIL_GROUPS):
                if win_caches is None:
                    o, l = dilated_prompt(q[:, :, g], kv_rows[g][:, :, 0], kv_rows[g][:, :, 1], biases[g], DIL_RATES[g])
                else:
                    o, l = dilated_sample(q[:, :, g], kv_rows[g], biases[g], DIL_RATES[g])
                outs.append(o)
                lses.append(l)
            dil = combine_dilations(outs, lses)
            mo = mem_attention(mq.reshape(b, L, MEM_HEADS, MEM_HEAD_DIM), mem_kv[layer])
            mix = jnp.concatenate([dil, mo], -1) @ w['b_w_out'][j]
        h = layer_norm(DN_ALPHA * h + mix, w['ln1_g'][layer], w['ln1_b'][layer])
        ff = moe(h, w['router_w'][layer], w['router_b'][layer], w['exp_w_gate_up'][layer],
                 w['exp_b_gate_up'][layer], w['exp_w_down'][layer], w['exp_b_down'][layer])
        h = layer_norm(DN_ALPHA * h + ff, w['ln2_g'][layer], w['ln2_b'][layer])
    return h, jnp.stack(new_ssm), jnp.stack(new_conv), new_win


def setup_inputs(seed: int = 0) -> dict:
    key = jax.random.key(seed)
    keys = iter(jax.random.split(key, 48))

    def nrm(shape, scale=1.0):
        return jax.random.normal(next(keys), shape, F32) * scale

    win_len = [min(wd, PAST_LEN) for wd in DIL_WINDOWS]
    dt0 = jnp.exp(jax.random.uniform(next(keys), (N_A_LAYERS, SSM_HEADS), F32, math.log(DT_MIN), math.log(DT_MAX)))
    return {
        'x_prompt': nrm((BATCH, SEQ, D_MODEL)),
        'x_sample': nrm((DEC_BATCH, DEC_SEQ, D_MODEL)),
        'mem_prompt': nrm((BATCH, MEM_TOKENS, D_MODEL)),
        'state_ssm': nrm((N_A_LAYERS, DEC_BATCH, SSM_HEADS, SSM_HEAD_DIM, SSM_STATE), 0.1),
        'state_conv': nrm((N_A_LAYERS, DEC_BATCH, SSM_CONV - 1, SSM_CONV_DIM)),
        'cache_win0_kv': nrm((DEC_BATCH, win_len[0], 2, DIL_SLOTS, DIL_HEAD_DIM)),
        'cache_win1_kv': nrm((DEC_BATCH, win_len[1], 2, DIL_SLOTS, DIL_HEAD_DIM)),
        'cache_win2_kv': nrm((DEC_BATCH, win_len[2], 2, DIL_SLOTS, DIL_HEAD_DIM)),
        'cache_mem_kv': nrm((DEPTH, DEC_BATCH, MEM_TOKENS, 2, MEM_HEADS, MEM_HEAD_DIM)),
        'a_w_in': nrm((N_A_LAYERS, D_MODEL, A_IN_DIM), D_MODEL ** -0.5),
        'a_conv_w': nrm((N_A_LAYERS, SSM_CONV, SSM_CONV_DIM), SSM_CONV ** -0.5),
        'a_conv_b': nrm((N_A_LAYERS, SSM_CONV_DIM), 0.02),
        'a_dt_bias': dt0 + jnp.log(-jnp.expm1(-dt0)),
        'a_a_log': jnp.log(jax.random.uniform(next(keys), (N_A_LAYERS, SSM_HEADS), F32, 1.0, 16.0)),
        'a_d_skip': 1.0 + nrm((N_A_LAYERS, SSM_HEADS), 0.1),
        'a_norm_w': 1.0 + nrm((N_A_LAYERS, SSM_D_INNER), 0.02),
        'a_w_out': nrm((N_A_LAYERS, A_OUT_IN, D_MODEL), A_OUT_IN ** -0.5 * DN_BETA),
        'w_kv_shared': nrm((D_MODEL, DIL_KV_DIM), D_MODEL ** -0.5),
        'rel_bias': nrm((REL_BUCKETS, DIL_GROUPS * DIL_SLOTS), 0.2),
        'b_w_in': nrm((N_B_LAYERS, D_MODEL, B_IN_DIM), D_MODEL ** -0.5),
        'b_w_out': nrm((N_B_LAYERS, B_OUT_IN, D_MODEL), B_OUT_IN ** -0.5 * DN_BETA),
        'w_mem_kv': nrm((DEPTH, D_MODEL, 2 * MEM_DIM), D_MODEL ** -0.5),
        'ln1_g': 1.0 + nrm((DEPTH, D_MODEL), 0.02),
        'ln1_b': nrm((DEPTH, D_MODEL), 0.02),
        'ln2_g': 1.0 + nrm((DEPTH, D_MODEL), 0.02),
        'ln2_b': nrm((DEPTH, D_MODEL), 0.02),
        'router_w': nrm((DEPTH, D_MODEL, N_EXPERTS), D_MODEL ** -0.5),
        'router_b': nrm((DEPTH, N_EXPERTS), 0.01),
        'exp_w_gate_up': nrm((DEPTH, N_EXPERTS, D_MODEL, 2 * D_EXPERT), D_MODEL ** -0.5),
        'exp_b_gate_up': nrm((DEPTH, N_EXPERTS, 2 * D_EXPERT), 0.02),
        'exp_w_down': nrm((DEPTH, N_EXPERTS, D_EXPERT, D_MODEL), D_EXPERT ** -0.5 * DN_BETA),
        'exp_b_down': nrm((DEPTH, N_EXPERTS, D_MODEL), 0.02),
    }


def reference(x_prompt, x_sample, mem_prompt, state_ssm, state_conv, cache_win0_kv, cache_win1_kv,
              cache_win2_kv, cache_mem_kv, a_w_in, a_conv_w, a_conv_b, a_dt_bias, a_a_log, a_d_skip,
              a_norm_w, a_w_out, w_kv_shared, rel_bias, b_w_in, b_w_out, w_mem_kv, ln1_g, ln1_b,
              ln2_g, ln2_b, router_w, router_b, exp_w_gate_up, exp_b_gate_up, exp_w_down, exp_b_down):
    w = dict(a_w_in=a_w_in, a_conv_w=a_conv_w, a_conv_b=a_conv_b, a_dt_bias=a_dt_bias, a_a_log=a_a_log,
             a_d_skip=a_d_skip, a_norm_w=a_norm_w, a_w_out=a_w_out, w_kv_shared=w_kv_shared,
             rel_bias=rel_bias, b_w_in=b_w_in, b_w_out=b_w_out, ln1_g=ln1_g, ln1_b=ln1_b,
             ln2_g=ln2_g, ln2_b=ln2_b, router_w=router_w, router_b=router_b,
             exp_w_gate_up=exp_w_gate_up, exp_b_gate_up=exp_b_gate_up, exp_w_down=exp_w_down,
             exp_b_down=exp_b_down)
    bp = x_prompt.shape[0]
    p_mem_kv = jnp.einsum('bmd,lde->lbme', mem_prompt, w_mem_kv).reshape(
        DEPTH, bp, MEM_TOKENS, 2, MEM_HEADS, MEM_HEAD_DIM)
    conv0 = jnp.zeros((N_A_LAYERS, bp, SSM_CONV - 1, SSM_CONV_DIM), x_prompt.dtype)
    ssm0 = jnp.zeros((N_A_LAYERS, bp, SSM_HEADS, SSM_HEAD_DIM, SSM_STATE), x_prompt.dtype)
    y_prompt, p_ssm, p_conv, p_win = trunk(x_prompt, p_mem_kv, conv0, ssm0, None, w)
    y_sample, s_ssm, s_conv, s_win = trunk(x_sample, cache_mem_kv, state_conv, state_ssm,
                                           [cache_win0_kv, cache_win1_kv, cache_win2_kv], w)
    return (y_prompt, y_sample, p_ssm, p_conv, p_win[0], p_win[1], p_win[2], p_mem_kv,
            s_ssm, s_conv, s_win[0], s_win[1], s_win[2])
```

```python
import functools
import math

import jax
import jax.numpy as jnp
import numpy as np
from jax import lax
from jax.experimental import pallas as pl
from jax.experimental.pallas import tpu as pltpu

F32 = jnp.float32
BF16 = jnp.bfloat16
HIGHEST = lax.Precision.HIGHEST

V7X_LANES = 128
V7X_SUBLANES = 8
V7X_VMEM_BYTES = 64 * 1024 * 1024
VMEM_LIMIT = V7X_VMEM_BYTES - 8 * 1024 * 1024

D_MODEL = 1024
DEPTH = 2
SSM_D_INNER = 2048
SSM_HEAD_DIM = 64
SSM_HEADS = 32
SSM_GROUPS = 4
SSM_STATE = 128
SSM_CONV = 4
SSM_CONV_DIM = SSM_D_INNER + 2 * SSM_GROUPS * SSM_STATE
SSM_CHUNK = 128
RMS_EPS = 1e-5
DIL_WINDOWS = (128, 512, 2048)
DIL_RATES = (1, 4, 16)
DIL_GROUPS = 3
DIL_SLOTS = 4
DIL_HEAD_DIM = 128
DIL_STEPS = 128
DIL_Q_DIM = DIL_GROUPS * DIL_SLOTS * DIL_HEAD_DIM
DIL_OUT_DIM = DIL_SLOTS * DIL_HEAD_DIM
DIL_KV_DIM = DIL_GROUPS * 2 * DIL_SLOTS * DIL_HEAD_DIM
DIL_SCALE = DIL_HEAD_DIM ** -0.5
MEM_TOKENS = 256
MEM_HEADS = 4
MEM_HEAD_DIM = 256
MEM_DIM = MEM_HEADS * MEM_HEAD_DIM
MEM_SCALE = MEM_HEAD_DIM ** -0.5
REL_BUCKETS = 32
REL_MAX_DIST = 2048
N_EXPERTS = 32
TOP_K = 4
D_EXPERT = D_MODEL
SWIGLU_LIMIT = 7.0
SWIGLU_ALPHA = 1.702
DN_ALPHA = (2 * DEPTH) ** 0.25
LN_EPS = 1e-5
NEG = -1e30

A_COLS = SSM_D_INNER + MEM_DIM + SSM_CONV_DIM + V7X_LANES
A_Z_BLK, A_MQ_BLK, A_XBC_BLK, A_DT_BLK = 0, 2, 1, 48
B_COLS = DIL_KV_DIM + MEM_DIM + DIL_Q_DIM
B_MQ_BLK = 3
B_Q_BLK0 = (DIL_KV_DIM + MEM_DIM) // DIL_OUT_DIM
B_BLKS = B_COLS // DIL_OUT_DIM

SAMPLE_ROWS = 8


def _cparams(semantics):
    return pltpu.CompilerParams(dimension_semantics=semantics, vmem_limit_bytes=VMEM_LIMIT)


def _dot(a, b, **kw):
    return jnp.dot(a, b, preferred_element_type=F32, **kw)


def _dot_nt(a, b):
    return lax.dot_general(a, b, (((1,), (1,)), ((), ())), preferred_element_type=F32)


def _sigmoid(x):
    return 1.0 / (1.0 + jnp.exp(-x))


def _linear_body(*refs, n_in, has_bias, has_ln, n_out, hi):
    xs = refs[:n_in]
    ws = refs[n_in:2 * n_in]
    pos = 2 * n_in
    bias = None
    if has_bias:
        bias = refs[pos]
        pos += 1
    if has_ln:
        resid, gam, bet = refs[pos:pos + 3]
        pos += 3
    outs = refs[pos:pos + n_out]
    acc = None
    for x, w in zip(xs, ws):
        if hi:
            part = _dot(x[...].astype(F32), w[...], precision=HIGHEST)
        else:
            part = _dot(x[...].astype(BF16), w[...])
        acc = part if acc is None else acc + part
    if has_bias:
        acc = acc + bias[...]
    if has_ln:
        r = DN_ALPHA * resid[...] + acc
        xc = r - jnp.mean(r, axis=-1, keepdims=True)
        var = jnp.mean(xc * xc, axis=-1, keepdims=True)
        acc = xc * lax.rsqrt(var + LN_EPS) * gam[...] + bet[...]
    for o in outs:
        o[...] = acc.astype(o.dtype)


def _linear(xs, ws, *, bm, bn, bias=None, ln=None, out_dtypes=(F32,), hi=False):
    m = xs[0][0].shape[0]
    n = ws[0].shape[1]
    assert m % bm == 0 and n % bn == 0
    in_specs, args = [], []
    for arr, k, cb in xs:
        in_specs.append(pl.BlockSpec((bm, k), lambda j, i, cb=cb: (i, cb)))
        args.append(arr)
    for (arr, k, cb), w in zip(xs, ws):
        assert w.shape == (k, n)
        in_specs.append(pl.BlockSpec((k, bn), lambda j, i: (0, j)))
        args.append(w)
    if bias is not None:
        in_specs.append(pl.BlockSpec((1, bn), lambda j, i: (0, j)))
        args.append(bias)
    if ln is not None:
        assert bn == n
        resid, gam, bet = ln
        in_specs.append(pl.BlockSpec((bm, n), lambda j, i: (i, 0)))
        in_specs.append(pl.BlockSpec((1, n), lambda j, i: (0, 0)))
        in_specs.append(pl.BlockSpec((1, n), lambda j, i: (0, 0)))
        args += [resid, gam, bet]
    out_shape = tuple(jax.ShapeDtypeStruct((m, n), dt) for dt in out_dtypes)
    out_specs = tuple(pl.BlockSpec((bm, bn), lambda j, i: (i, j)) for _ in out_dtypes)
    body = functools.partial(_linear_body, n_in=len(xs), has_bias=bias is not None,
                             has_ln=ln is not None, n_out=len(out_dtypes), hi=hi)
    res = pl.pallas_call(
        body, grid=(n // bn, m // bm), in_specs=in_specs, out_specs=out_specs, out_shape=out_shape,
        compiler_params=_cparams(("parallel", "parallel")), name="linear")(*args)
    return res[0] if len(out_dtypes) == 1 else res


def _ssd_body(xbc_ref, z_ref, dt_ref, s0_ref, c0_ref, cw_ref, cb_ref, dtb_ref, alog_ref, dskip_ref,
              nw_ref, e_ref, y_ref, sout_ref, cout_ref, st_ref, xpad_ref, xc_ref, ys_ref,
              *, q, lb, nvalid, nchunks):
    c = pl.program_id(1)
    npair = SSM_HEADS // 2

    @pl.when(c == 0)
    def _init():
        for p in range(npair):
            s2 = s0_ref[2 * p:2 * p + 2].reshape(2 * SSM_HEAD_DIM, SSM_STATE)
            st_ref[p] = s2.T
        xpad_ref[0:8, :] = c0_ref[...]

    cwid = 512
    for cbk in range(SSM_CONV_DIM // cwid):
        cs = slice(cbk * cwid, (cbk + 1) * cwid)
        x_in = xbc_ref[:, cs]
        if lb < q:
            x_in = jnp.concatenate([x_in, jnp.zeros((q - lb, cwid), F32)], axis=0)
        xpad_ref[8:8 + q, cs] = x_in
        acc = cb_ref[:, cs] + x_in * cw_ref[SSM_CONV - 1:SSM_CONV, cs]
        for k in range(SSM_CONV - 1):
            acc = acc + xpad_ref[5 + k:5 + k + q, cs] * cw_ref[k:k + 1, cs]
        xc_ref[:, cs] = acc * _sigmoid(acc)

    row = lax.broadcasted_iota(jnp.int32, (q, q), 0)
    col = lax.broadcasted_iota(jnp.int32, (q, q), 1)
    causal = row >= col
    lane = lax.broadcasted_iota(jnp.int32, (q, 2 * SSM_HEAD_DIM), 1)
    lo = lane < SSM_HEAD_DIM

    dt_raw = dt_ref[...]
    if lb < q:
        dt_raw = jnp.concatenate([dt_raw, jnp.zeros((q - lb, V7X_LANES), F32)], axis=0)
    pre = dt_raw + dtb_ref[...]
    dt = jnp.maximum(pre, 0.0) + jnp.log(1.0 + jnp.exp(-jnp.abs(pre)))
    if nvalid < q:
        dt = jnp.where(lax.broadcasted_iota(jnp.int32, (q, V7X_LANES), 0) < nvalid, dt, 0.0)
    a = -jnp.exp(alog_ref[...])
    da = dt * a
    acs = _dot(causal.astype(F32), da, precision=HIGHEST)
    acs_t = acs.T
    dt_t = dt.T
    eacs = jnp.exp(acs)
    w_t = dt_t * jnp.exp(acs_t[:, q - 1:q] - acs_t)
    dec_row = jnp.broadcast_to(jnp.exp(acs[q - 1:q, :]), (V7X_SUBLANES, V7X_LANES))
    dec_exp = _dot(dec_row, e_ref[...], precision=HIGHEST)[0:1]

    def blockdiag(v):
        z = jnp.zeros_like(v)
        return jnp.concatenate([jnp.where(lo, v, z), jnp.where(lo, z, v)], axis=0).astype(BF16)

    for g in range(SSM_GROUPS):
        bc = xc_ref[:, SSM_D_INNER + g * SSM_STATE:SSM_D_INNER + (g + 1) * SSM_STATE]
        cc = xc_ref[:, SSM_D_INNER + (SSM_GROUPS + g) * SSM_STATE:SSM_D_INNER + (SSM_GROUPS + g + 1) * SSM_STATE]
        gm = jnp.where(causal, _dot_nt(cc.astype(BF16), bc.astype(BF16)), 0.0)
        bc_t = bc.T
        for pp in range(SSM_HEADS // SSM_GROUPS // 2):
            p = g * (SSM_HEADS // SSM_GROUPS // 2) + pp
            h0 = 2 * p
            ps = slice(h0 * SSM_HEAD_DIM, (h0 + 2) * SSM_HEAD_DIM)
            bd_x = blockdiag(xc_ref[:, ps])
            st = st_ref[p]
            parts = []
            for j in range(2):
                h = h0 + j
                seg = acs[:, h:h + 1] - acs_t[h:h + 1, :]
                parts.append((gm * jnp.exp(jnp.minimum(seg, 0.0)) * dt_t[h:h + 1, :]).astype(BF16))
            for j in range(2):
                h = h0 + j
                parts.append((cc * eacs[:, h:h + 1]).astype(BF16))
            lhs = jnp.concatenate(parts, axis=1)
            rhs = jnp.concatenate([bd_x, blockdiag(st)], axis=0)
            ys_ref[:, ps] = _dot(lhs, rhs)
            lhs_s = jnp.concatenate([(bc_t * w_t[h0:h0 + 1, :]).astype(BF16),
                                     (bc_t * w_t[h0 + 1:h0 + 2, :]).astype(BF16)], axis=1)
            st_ref[p] = st * dec_exp[:, ps] + _dot(lhs_s, bd_x)

    gw = SSM_D_INNER // SSM_GROUPS
    for g in range(SSM_GROUPS):
        gs = slice(g * gw, (g + 1) * gw)
        yv = ys_ref[0:lb, gs] + xc_ref[0:lb, gs] * dskip_ref[:, gs]
        zz = z_ref[:, gs]
        hf = yv * (zz * _sigmoid(zz))
        ms = jnp.mean(hf * hf, axis=-1, keepdims=True)
        y_ref[:, gs] = (hf * lax.rsqrt(ms + RMS_EPS) * nw_ref[:, gs]).astype(y_ref.dtype)

    @pl.when(c == nchunks - 1)
    def _fin():
        for p in range(npair):
            sout_ref[2 * p:2 * p + 2] = st_ref[p].T.reshape(2, SSM_HEAD_DIM, SSM_STATE)
        cout_ref[...] = xpad_ref[nvalid:nvalid + 8, :]

    xpad_ref[0:8, :] = xpad_ref[q:q + 8, :]


def _ssd(proj, s0, c0p, wd, *, lb, nvalid):
    b, l, _ = proj.shape
    q = SSM_CHUNK
    nchunks = l // lb
    assert l % lb == 0 and (lb == q or nchunks == 1)
    body = functools.partial(_ssd_body, q=q, lb=lb, nvalid=nvalid, nchunks=nchunks)
    full = lambda shape: pl.BlockSpec(shape, lambda bi, ci: (0,) * len(shape))
    in_specs = [
        pl.BlockSpec((None, lb, SSM_CONV_DIM), lambda bi, ci: (bi, ci, A_XBC_BLK)),
        pl.BlockSpec((None, lb, SSM_D_INNER), lambda bi, ci: (bi, ci, A_Z_BLK)),
        pl.BlockSpec((None, lb, V7X_LANES), lambda bi, ci: (bi, ci, A_DT_BLK)),
        pl.BlockSpec((None, SSM_HEADS, SSM_HEAD_DIM, SSM_STATE), lambda bi, ci: (bi, 0, 0, 0)),
        pl.BlockSpec((None, 8, SSM_CONV_DIM), lambda bi, ci: (bi, 0, 0)),
        full((SSM_CONV, SSM_CONV_DIM)), full((1, SSM_CONV_DIM)), full((1, V7X_LANES)), full((1, V7X_LANES)),
        full((1, SSM_D_INNER)), full((1, SSM_D_INNER)), full((V7X_LANES, SSM_D_INNER)),
    ]
    out_shape = (jax.ShapeDtypeStruct((b, l, SSM_D_INNER), BF16),
                 jax.ShapeDtypeStruct((b, SSM_HEADS, SSM_HEAD_DIM, SSM_STATE), F32),
                 jax.ShapeDtypeStruct((b, 8, SSM_CONV_DIM), F32))
    out_specs = (pl.BlockSpec((None, lb, SSM_D_INNER), lambda bi, ci: (bi, ci, 0)),
                 pl.BlockSpec((None, SSM_HEADS, SSM_HEAD_DIM, SSM_STATE), lambda bi, ci: (bi, 0, 0, 0)),
                 pl.BlockSpec((None, 8, SSM_CONV_DIM), lambda bi, ci: (bi, 0, 0)))
    scratch = [pltpu.VMEM((SSM_HEADS // 2, SSM_STATE, 2 * SSM_HEAD_DIM), F32),
               pltpu.VMEM((q + 8, SSM_CONV_DIM), F32),
               pltpu.VMEM((q, SSM_CONV_DIM), F32),
               pltpu.VMEM((q, SSM_D_INNER), F32)]
    return pl.pallas_call(
        body, grid=(b, nchunks), in_specs=in_specs, out_specs=out_specs, out_shape=out_shape,
        scratch_shapes=scratch, compiler_params=_cparams(("parallel", "arbitrary")), name="ssd")(
            proj, proj, proj, s0, c0p, wd["conv_w"], wd["conv_b"], wd["dt_bias"], wd["a_log"],
            wd["d_skip"], wd["norm_w"], wd["head_expand"])


def _memattn_body(q_ref, kv_ref, o_ref):
    for h in range(MEM_HEADS):
        hs = slice(h * MEM_HEAD_DIM, (h + 1) * MEM_HEAD_DIM)
        qh = q_ref[:, hs].astype(BF16)
        kh = kv_ref[:, hs].astype(BF16)
        vh = kv_ref[:, MEM_DIM + h * MEM_HEAD_DIM:MEM_DIM + (h + 1) * MEM_HEAD_DIM].astype(BF16)
        s = _dot_nt(qh, kh) * MEM_SCALE
        e = jnp.exp(s - jnp.max(s, axis=-1, keepdims=True))
        p = e / jnp.sum(e, axis=-1, keepdims=True)
        o_ref[:, hs] = _dot(p.astype(BF16), vh).astype(o_ref.dtype)


def _memattn(proj, mq_blk, mem_kv, *, tl, nblk):
    b = proj.shape[0]
    return pl.pallas_call(
        _memattn_body, grid=(b, nblk),
        in_specs=[pl.BlockSpec((None, tl, MEM_DIM), lambda bi, i: (bi, i, mq_blk)),
                  pl.BlockSpec((None, MEM_TOKENS, 2 * MEM_DIM), lambda bi, i: (bi, 0, 0))],
        out_specs=pl.BlockSpec((None, tl, MEM_DIM), lambda bi, i: (bi, i, 0)),
        out_shape=jax.ShapeDtypeStruct((b, tl * nblk, MEM_DIM), BF16),
        compiler_params=_cparams(("parallel", "parallel")), name="memattn")(proj, mem_kv)


def _t5_bucket_np(dist):
    max_exact = REL_BUCKETS // 2
    d_f = np.maximum(dist, 1).astype(np.float32)
    large = max_exact + (np.log(d_f / np.float32(max_exact)) / np.float32(math.log(REL_MAX_DIST / max_exact))
                         * np.float32(REL_BUCKETS - max_exact)).astype(np.int32)
    large = np.minimum(large, REL_BUCKETS - 1)
    return np.where(dist < max_exact, dist, large)


def _group_bias(rel_bias, g):
    bucket = _t5_bucket_np(DIL_RATES[g] * np.arange(DIL_STEPS + 1))
    return rel_bias[bucket][:, g * DIL_SLOTS:(g + 1) * DIL_SLOTS]


def _dilp_body(q_ref, kc_ref, vc_ref, kp_ref, vp_ref, bias_ref, o_ref, l_ref, *, tb):
    n = pl.program_id(1)
    s_ = DIL_STEPS
    for j in range(tb // s_):
        rs = slice(j * s_, (j + 1) * s_)
        for h in range(DIL_SLOTS):
            hs = slice(h * DIL_HEAD_DIM, (h + 1) * DIL_HEAD_DIM)
            qh = q_ref[rs, hs].astype(BF16)
            if j == 0:
                kp, vp = kp_ref[:, hs], vp_ref[:, hs]
            else:
                ps = slice((j - 1) * s_, j * s_)
                kp, vp = kc_ref[ps, hs], vc_ref[ps, hs]
            kc, vc = kc_ref[rs, hs], vc_ref[rs, hs]
            sp = _dot_nt(qh, kp.astype(BF16)) * DIL_SCALE + bias_ref[h, :, 0:s_]
            sc = _dot_nt(qh, kc.astype(BF16)) * DIL_SCALE + bias_ref[h, :, s_:2 * s_]
            if j == 0:
                sp = jnp.where(n > 0, sp, NEG)
            m = jnp.maximum(jnp.max(sp, axis=-1, keepdims=True), jnp.max(sc, axis=-1, keepdims=True))
            ep = jnp.exp(sp - m)
            ec = jnp.exp(sc - m)
            den = jnp.sum(ep, axis=-1, keepdims=True) + jnp.sum(ec, axis=-1, keepdims=True)
            o = _dot((ep / den).astype(BF16), vp.astype(BF16)) + _dot((ec / den).astype(BF16), vc.astype(BF16))
            o_ref[rs, hs] = o
            l_ref[rs, hs] = jnp.broadcast_to(m + jnp.log(den), (s_, DIL_HEAD_DIM))


def _dil_prompt(projb, bias_qk, g, *, tb):
    t = projb.shape[0]
    r = DIL_RATES[g]
    tr = t // r
    assert tr % tb == 0 and tb % DIL_STEPS == 0
    view = projb.reshape(tr, r * B_COLS)
    w = DIL_OUT_DIM
    kblk = lambda c: c * B_BLKS + 2 * g
    per = tb // DIL_STEPS
    in_specs = [
        pl.BlockSpec((tb, w), lambda c, n: (n, c * B_BLKS + B_Q_BLK0 + g)),
        pl.BlockSpec((tb, w), lambda c, n: (n, kblk(c))),
        pl.BlockSpec((tb, w), lambda c, n: (n, kblk(c) + 1)),
        pl.BlockSpec((DIL_STEPS, w), lambda c, n: (jnp.maximum(n * per - 1, 0), kblk(c))),
        pl.BlockSpec((DIL_STEPS, w), lambda c, n: (jnp.maximum(n * per - 1, 0), kblk(c) + 1)),
        pl.BlockSpec((DIL_SLOTS, DIL_STEPS, 2 * DIL_STEPS), lambda c, n: (0, 0, 0)),
    ]
    o, l = pl.pallas_call(
        functools.partial(_dilp_body, tb=tb), grid=(r, tr // tb), in_specs=in_specs,
        out_specs=(pl.BlockSpec((tb, w), lambda c, n: (n, c)), pl.BlockSpec((tb, w), lambda c, n: (n, c))),
        out_shape=(jax.ShapeDtypeStruct((tr, r * w), F32), jax.ShapeDtypeStruct((tr, r * w), F32)),
        compiler_params=_cparams(("parallel", "parallel")), name="dil_prompt")(
            view, view, view, view, view, bias_qk)
    return o.reshape(t, w), l.reshape(t, w)


def _prompt_bias_table(bias):
    s_ = DIL_STEPS
    step = s_ + np.arange(s_)[:, None] - np.arange(2 * s_)[None, :]
    band = (step >= 0) & (step <= s_)
    tab = jnp.moveaxis(bias[np.clip(step, 0, s_)], -1, 0).astype(F32)
    return jnp.where(band[None], tab, NEG)


def _dils_body(q_ref, c_ref, n_ref, tb_ref, o_ref, l_ref, co_ref, kv_ref, *, w):
    pad = V7X_LANES
    kv_ref[0:w, :] = c_ref[...]
    kv_ref[w:w + SAMPLE_ROWS, :] = n_ref[...]
    kv_ref[w + SAMPLE_ROWS:w + pad, :] = jnp.zeros((pad - SAMPLE_ROWS, 2 * DIL_OUT_DIM), F32)
    for h in range(DIL_SLOTS):
        hs = slice(h * DIL_HEAD_DIM, (h + 1) * DIL_HEAD_DIM)
        qh = q_ref[:, hs].astype(BF16)
        kh = kv_ref[:, hs].astype(BF16)
        vh = kv_ref[:, DIL_OUT_DIM + h * DIL_HEAD_DIM:DIL_OUT_DIM + (h + 1) * DIL_HEAD_DIM].astype(BF16)
        s = _dot_nt(qh, kh) * DIL_SCALE + tb_ref[h]
        m = jnp.max(s, axis=-1, keepdims=True)
        e = jnp.exp(s - m)
        den = jnp.sum(e, axis=-1, keepdims=True)
        o_ref[:, hs] = _dot((e / den).astype(BF16), vh)
        l_ref[:, hs] = jnp.broadcast_to(m + jnp.log(den), (SAMPLE_ROWS, DIL_HEAD_DIM))
    nnew = SAMPLE_ROWS // 2
    co_ref[...] = kv_ref[nnew:nnew + w, :]


def _dil_sample(projb, cache, bias, g):
    b = projb.shape[0]
    w = cache.shape[1]
    r = DIL_RATES[g]
    nnew = SAMPLE_ROWS // 2
    pad = V7X_LANES
    dist = (w + np.arange(SAMPLE_ROWS))[:, None] - np.arange(w + pad)[None, :]
    ok = (dist >= 0) & (dist % r == 0) & (dist // r <= DIL_STEPS) & (np.arange(w + pad)[None, :] < w + nnew)
    ok = ok | ((np.arange(SAMPLE_ROWS)[:, None] >= nnew) & (dist == 0))
    tab = jnp.moveaxis(bias[np.clip(dist // r, 0, DIL_STEPS)], -1, 0).astype(F32)
    tab = jnp.where(ok[None], tab, NEG)
    wd = DIL_OUT_DIM
    o, l, co = pl.pallas_call(
        functools.partial(_dils_body, w=w), grid=(b,),
        in_specs=[pl.BlockSpec((None, SAMPLE_ROWS, wd), lambda i: (i, 0, B_Q_BLK0 + g)),
                  pl.BlockSpec((None, w, 2 * wd), lambda i: (i, 0, 0)),
                  pl.BlockSpec((None, SAMPLE_ROWS, 2 * wd), lambda i: (i, 0, g)),
                  pl.BlockSpec((DIL_SLOTS, SAMPLE_ROWS, w + pad), lambda i: (0, 0, 0))],
        out_specs=(pl.BlockSpec((None, SAMPLE_ROWS, wd), lambda i: (i, 0, 0)),
                   pl.BlockSpec((None, SAMPLE_ROWS, wd), lambda i: (i, 0, 0)),
                   pl.BlockSpec((None, w, 2 * wd), lambda i: (i, 0, 0))),
        out_shape=(jax.ShapeDtypeStruct((b, SAMPLE_ROWS, wd), F32),
                   jax.ShapeDtypeStruct((b, SAMPLE_ROWS, wd), F32),
                   jax.ShapeDtypeStruct((b, w, 2 * wd), F32)),
        scratch_shapes=[pltpu.VMEM((w + pad, 2 * wd), F32)],
        compiler_params=_cparams(("parallel",)), name="dil_sample")(projb, cache, projb, tab)
    return o, l, co


def _dilcomb_body(o0, o1, o2, l0, l1, l2, out_ref):
    m = jnp.maximum(jnp.maximum(l0[...], l1[...]), l2[...])
    e0 = jnp.exp(l0[...] - m)
    e1 = jnp.exp(l1[...] - m)
    e2 = jnp.exp(l2[...] - m)
    out = (e0 * o0[...] + e1 * o1[...] + e2 * o2[...]) / (e0 + e1 + e2)
    out_ref[...] = out.astype(out_ref.dtype)


def _dil_combine(outs, lses, *, bm):
    t, w = outs[0].shape
    spec = pl.BlockSpec((bm, w), lambda i: (i, 0))
    return pl.pallas_call(
        _dilcomb_body, grid=(t // bm,), in_specs=[spec] * 6, out_specs=spec,
        out_shape=jax.ShapeDtypeStruct((t, w), BF16),
        compiler_params=_cparams(("parallel",)), name="dil_combine")(*outs, *lses)


def _route(logits, tm):
    t = logits.shape[0]
    a = t * TOP_K
    top_v, top_i = lax.top_k(logits, TOP_K)
    gates = jax.nn.softmax(top_v, axis=-1).reshape(-1)
    flat_e = top_i.reshape(-1).astype(jnp.int32)
    onehot = (flat_e[:, None] == jnp.arange(N_EXPERTS, dtype=jnp.int32)[None, :]).astype(jnp.int32)
    csum = jnp.cumsum(onehot, axis=0)
    counts = csum[-1]
    rank = jnp.take_along_axis(csum, flat_e[:, None], axis=1)[:, 0] - 1
    padded = (counts + tm - 1) // tm * tm
    pad_end = jnp.cumsum(padded)
    pad_start = pad_end - padded
    start = jnp.cumsum(counts) - counts
    pos = (pad_start[flat_e] + rank).astype(jnp.int32)
    n_blocks = a // tm + N_EXPERTS
    order = jnp.argsort(flat_e, stable=True).astype(jnp.int32)
    block_e = jnp.minimum(jnp.searchsorted(pad_end, jnp.arange(n_blocks) * tm, side="right"),
                          N_EXPERTS - 1).astype(jnp.int32)
    nvalid = (pad_end[-1] // tm).astype(jnp.int32).reshape(1)
    block_e = jnp.where(jnp.arange(n_blocks) < nvalid[0], block_e, block_e[nvalid[0] - 1])
    first = jnp.concatenate([jnp.ones((1,), jnp.int32), (block_e[1:] != block_e[:-1]).astype(jnp.int32)])
    s = jnp.arange(n_blocks * tm, dtype=jnp.int32)
    e_s = block_e[s // tm]
    idx = s - pad_start[e_s]
    valid = (idx < counts[e_s]) & (s < pad_end[-1])
    a_s = order[jnp.clip(start[e_s] + idx, 0, a - 1)]
    slot_tok = jnp.where(valid, a_s // TOP_K, 0).astype(jnp.int32)
    slot_gate = jnp.where(valid, gates[a_s], 0.0).astype(F32)
    return slot_tok, slot_gate, block_e, first, nvalid, pos


def _expert_body(be_ref, first_ref, nv_ref, tok_ref, x_hbm, wgu_ref, bgu_ref, wd_ref, bd_ref, gate_ref,
                 y_ref, xbuf, sem, wgu_bf, wd_bf, *, tm):
    i = pl.program_id(0)
    nv = nv_ref[0]

    def issue(blk, slot):
        base = blk * tm

        def body(r, carry):
            tok = tok_ref[base + r]
            pltpu.make_async_copy(x_hbm.at[pl.ds(tok, 1)], xbuf.at[slot, pl.ds(r, 1)], sem.at[slot]).start()
            return carry

        lax.fori_loop(0, tm, body, 0, unroll=8)

    slot = lax.rem(i, 2)

    @pl.when(i == 0)
    def _prologue():
        issue(0, 0)

    @pl.when(i + 1 < nv)
    def _prefetch():
        issue(i + 1, 1 - slot)

    @pl.when(i < nv)
    def _compute():
        pltpu.make_async_copy(x_hbm.at[pl.ds(0, tm)], xbuf.at[slot], sem.at[slot]).wait()

        @pl.when(first_ref[i] == 1)
        def _cast():
            wgu_bf[...] = wgu_ref[...].astype(BF16)
            wd_bf[...] = wd_ref[...].astype(BF16)

        x = xbuf[slot].astype(BF16)
        gu = _dot(x, wgu_bf[...]) + bgu_ref[...]
        gl = jnp.minimum(gu[:, :D_EXPERT], SWIGLU_LIMIT)
        up = jnp.clip(gu[:, D_EXPERT:], -SWIGLU_LIMIT, SWIGLU_LIMIT)
        act = gl * _sigmoid(SWIGLU_ALPHA * gl) * (up + 1.0)
        y = _dot(act.astype(BF16), wd_bf[...]) + bd_ref[...]
        y_ref[...] = y * jnp.tile(gate_ref[...], (1, D_MODEL // V7X_LANES))

    @pl.when(i >= nv)
    def _unused():
        y_ref[...] = jnp.zeros_like(y_ref)


def _experts(x, slot_tok, gate_b, block_e, first, nvalid, w_gu, b_gu, w_down, b_down, *, tm):
    n_blocks = block_e.shape[0]
    d = D_MODEL
    last = lambda nv, i: jnp.minimum(i, nv[0] - 1)
    grid_spec = pltpu.PrefetchScalarGridSpec(
        num_scalar_prefetch=4, grid=(n_blocks,),
        in_specs=[
            pl.BlockSpec(memory_space=pl.ANY),
            pl.BlockSpec((None, d, 2 * D_EXPERT), lambda i, be, fi, nv, tok: (be[i], 0, 0)),
            pl.BlockSpec((None, 1, 2 * D_EXPERT), lambda i, be, fi, nv, tok: (be[i], 0, 0)),
            pl.BlockSpec((None, D_EXPERT, d), lambda i, be, fi, nv, tok: (be[i], 0, 0)),
            pl.BlockSpec((None, 1, d), lambda i, be, fi, nv, tok: (be[i], 0, 0)),
            pl.BlockSpec((tm, V7X_LANES), lambda i, be, fi, nv, tok: (last(nv, i), 0)),
        ],
        out_specs=pl.BlockSpec((tm, d), lambda i, be, fi, nv, tok: (i, 0)),
        scratch_shapes=[pltpu.VMEM((2, tm, d), F32), pltpu.SemaphoreType.DMA((2,)),
                        pltpu.VMEM((d, 2 * D_EXPERT), BF16), pltpu.VMEM((D_EXPERT, d), BF16)])
    return pl.pallas_call(
        functools.partial(_expert_body, tm=tm), grid_spec=grid_spec,
        out_shape=jax.ShapeDtypeStruct((n_blocks * tm, d), F32),
        compiler_params=_cparams(("arbitrary",)), name="experts")(
            block_e, first, nvalid, slot_tok, x, w_gu, b_gu.reshape(N_EXPERTS, 1, -1), w_down,
            b_down.reshape(N_EXPERTS, 1, -1), gate_b)


def _combine_body(pos_ref, ys_hbm, h_ref, gam_ref, bet_ref, o_ref, buf, sem, *, tt, nsteps):
    i = pl.program_id(0)
    nrow = TOP_K * tt

    def issue(step, slot):
        base = step * nrow

        def body(j, carry):
            p = pos_ref[base + j]
            dst = (j & (TOP_K - 1)) * tt + (j >> 2)
            pltpu.make_async_copy(ys_hbm.at[pl.ds(p, 1)], buf.at[slot, pl.ds(dst, 1)], sem.at[slot]).start()
            return carry

        lax.fori_loop(0, nrow, body, 0, unroll=8)

    slot = lax.rem(i, 2)

    @pl.when(i == 0)
    def _prologue():
        issue(0, 0)

    @pl.when(i + 1 < nsteps)
    def _prefetch():
        issue(i + 1, 1 - slot)

    pltpu.make_async_copy(ys_hbm.at[pl.ds(0, nrow)], buf.at[slot], sem.at[slot]).wait()
    ff = buf[slot, 0:tt, :]
    for k in range(1, TOP_K):
        ff = ff + buf[slot, k * tt:(k + 1) * tt, :]
    r = DN_ALPHA * h_ref[...] + ff
    xc = r - jnp.mean(r, axis=-1, keepdims=True)
    var = jnp.mean(xc * xc, axis=-1, keepdims=True)
    o_ref[...] = xc * lax.rsqrt(var + LN_EPS) * gam_ref[...] + bet_ref[...]


def _combine(ys, pos, h, gam, bet, *, tt):
    t, d = h.shape
    nsteps = t // tt
    grid_spec = pltpu.PrefetchScalarGridSpec(
        num_scalar_prefetch=1, grid=(nsteps,),
        in_specs=[pl.BlockSpec(memory_space=pl.ANY),
                  pl.BlockSpec((tt, d), lambda i, pos: (i, 0)),
                  pl.BlockSpec((1, d), lambda i, pos: (0, 0)),
                  pl.BlockSpec((1, d), lambda i, pos: (0, 0))],
        out_specs=pl.BlockSpec((tt, d), lambda i, pos: (i, 0)),
        scratch_shapes=[pltpu.VMEM((2, TOP_K * tt, d), F32), pltpu.SemaphoreType.DMA((2,))])
    return pl.pallas_call(
        functools.partial(_combine_body, tt=tt, nsteps=nsteps), grid_spec=grid_spec,
        out_shape=jax.ShapeDtypeStruct((t, d), F32),
        compiler_params=_cparams(("arbitrary",)), name="moe_combine")(pos, ys, h, gam, bet)


def _moe_ln(h, lw, *, tm, tt, bm):
    t = h.shape[0]
    logits = _linear([(h, D_MODEL, 0)], [lw["router_w"]], bias=lw["router_b"], bm=bm, bn=V7X_LANES, hi=True)
    slot_tok, slot_gate, block_e, first, nvalid, pos = _route(logits[:, :N_EXPERTS], tm)
    gate_b = jnp.broadcast_to(slot_gate[:, None], (slot_gate.shape[0], V7X_LANES))
    ys = _experts(h, slot_tok, gate_b, block_e, first, nvalid, lw["w_gu"], lw["b_gu"], lw["w_down"],
                  lw["b_down"], tm=tm)
    return _combine(ys, pos, h, lw["ln2_g"], lw["ln2_b"], tt=tt)


def _prep_weights(a_w_in, a_conv_w, a_conv_b, a_dt_bias, a_a_log, a_d_skip, a_norm_w, a_w_out, w_kv_shared,
                  b_w_in, b_w_out, w_mem_kv, ln1_g, ln1_b, ln2_g, ln2_b, router_w, router_b,
                  exp_w_gate_up, exp_b_gate_up, exp_w_down, exp_b_down):
    wi = a_w_in[0]
    o_xbc = SSM_D_INNER
    o_dt = SSM_D_INNER + SSM_CONV_DIM
    o_mq = o_dt + SSM_HEADS
    padl = V7X_LANES - SSM_HEADS
    a_in = jnp.concatenate([wi[:, :o_xbc], wi[:, o_mq:], wi[:, o_xbc:o_dt],
                            jnp.pad(wi[:, o_dt:o_mq], ((0, 0), (0, padl)))], axis=1).astype(BF16)
    bi = b_w_in[0]
    b_in = jnp.concatenate([w_kv_shared, bi[:, DIL_Q_DIM:], bi[:, :DIL_Q_DIM]], axis=1).astype(BF16)
    head = np.arange(SSM_D_INNER) // SSM_HEAD_DIM
    head_expand = jnp.asarray((np.arange(V7X_LANES)[:, None] == head[None, :]).astype(np.float32))
    ssd = dict(conv_w=a_conv_w[0], conv_b=a_conv_b[0][None], dt_bias=jnp.pad(a_dt_bias[0], (0, padl))[None],
               a_log=jnp.pad(a_a_log[0], (0, padl))[None], d_skip=jnp.repeat(a_d_skip[0], SSM_HEAD_DIM)[None],
               norm_w=a_norm_w[0][None], head_expand=head_expand)
    layers = []
    for l in range(DEPTH):
        layers.append(dict(
            ln1_g=ln1_g[l][None], ln1_b=ln1_b[l][None], ln2_g=ln2_g[l][None], ln2_b=ln2_b[l][None],
            router_w=jnp.pad(router_w[l], ((0, 0), (0, V7X_LANES - N_EXPERTS))),
            router_b=jnp.pad(router_b[l], (0, V7X_LANES - N_EXPERTS), constant_values=NEG)[None],
            w_gu=exp_w_gate_up[l], b_gu=exp_b_gate_up[l], w_down=exp_w_down[l], b_down=exp_b_down[l]))
    return dict(a_in=a_in, b_in=b_in, ssd=ssd, layers=layers,
                a_out_y=a_w_out[0][:SSM_D_INNER].astype(BF16), a_out_m=a_w_out[0][SSM_D_INNER:].astype(BF16),
                b_out_d=b_w_out[0][:DIL_OUT_DIM].astype(BF16), b_out_m=b_w_out[0][DIL_OUT_DIM:].astype(BF16),
                mem_kv=w_mem_kv.astype(BF16))


def _trunk(x, mem_kv, c0p, s0, wts, rel_bias, win_caches, cfg):
    b, l, d = x.shape
    t = b * l
    bm = cfg["bm"]
    x2 = x.reshape(t, d)

    proj = _linear([(x2, d, 0)], [wts["a_in"]], bm=bm, bn=cfg["bn_a"])
    proj3 = proj.reshape(b, l, A_COLS)
    y, s_new, c_new = _ssd(proj3, s0, c0p, wts["ssd"], lb=cfg["lb"], nvalid=cfg["nvalid"])
    mo = _memattn(proj3, A_MQ_BLK, mem_kv[0], tl=cfg["tl"], nblk=l // cfg["tl"])
    lw = wts["layers"][0]
    h = _linear([(y.reshape(t, SSM_D_INNER), SSM_D_INNER, 0), (mo.reshape(t, MEM_DIM), MEM_DIM, 0)],
                [wts["a_out_y"], wts["a_out_m"]], ln=(x2, lw["ln1_g"], lw["ln1_b"]), bm=cfg["bm_ln"], bn=d)
    h = _moe_ln(h, lw, tm=cfg["tm"], tt=cfg["tt"], bm=bm)

    projb = _linear([(h, d, 0)], [wts["b_in"]], bm=bm, bn=cfg["bn_b"])
    outs, lses, new_win = [], [], []
    for g in range(DIL_GROUPS):
        bias = _group_bias(rel_bias, g)
        if win_caches is None:
            o, ls = _dil_prompt(projb, _prompt_bias_table(bias), g, tb=cfg["tb"][g])
            wg = min(DIL_WINDOWS[g], t)
            new_win.append(projb[t - wg:, g * 2 * DIL_OUT_DIM:(g + 1) * 2 * DIL_OUT_DIM])
        else:
            o, ls, co = _dil_sample(projb.reshape(b, l, B_COLS), win_caches[g], bias, g)
            o, ls = o.reshape(t, DIL_OUT_DIM), ls.reshape(t, DIL_OUT_DIM)
            new_win.append(co)
        outs.append(o)
        lses.append(ls)
    dil = _dil_combine(outs, lses, bm=cfg["bm_c"])
    mo = _memattn(projb.reshape(b, l, B_COLS), B_MQ_BLK, mem_kv[1], tl=cfg["tl"], nblk=l // cfg["tl"])
    lw = wts["layers"][1]
    h2 = _linear([(dil, DIL_OUT_DIM, 0), (mo.reshape(t, MEM_DIM), MEM_DIM, 0)],
                 [wts["b_out_d"], wts["b_out_m"]], ln=(h, lw["ln1_g"], lw["ln1_b"]), bm=cfg["bm_ln"], bn=d)
    h2 = _moe_ln(h2, lw, tm=cfg["tm"], tt=cfg["tt"], bm=bm)
    return h2, s_new, c_new, new_win


PROMPT_CFG = dict(bm=1024, bn_a=896, bn_b=512, bm_ln=512, lb=SSM_CHUNK, nvalid=SSM_CHUNK, tl=512,
                  tm=256, tt=128, tb=(512, 512, 512), bm_c=1024)
SAMPLE_CFG = dict(bm=256, bn_a=896, bn_b=512, bm_ln=256, lb=SAMPLE_ROWS, nvalid=SAMPLE_ROWS // 2,
                  tl=SAMPLE_ROWS, tm=128, tt=128, tb=None, bm_c=256)


def kernel(x_prompt, x_sample, mem_prompt, state_ssm, state_conv, cache_win0_kv, cache_win1_kv, cache_win2_kv,
           cache_mem_kv, a_w_in, a_conv_w, a_conv_b, a_dt_bias, a_a_log, a_d_skip, a_norm_w, a_w_out,
           w_kv_shared, rel_bias, b_w_in, b_w_out, w_mem_kv, ln1_g, ln1_b, ln2_g, ln2_b, router_w, router_b,
           exp_w_gate_up, exp_b_gate_up, exp_w_down, exp_b_down):
    wts = _prep_weights(a_w_in, a_conv_w, a_conv_b, a_dt_bias, a_a_log, a_d_skip, a_norm_w, a_w_out,
                        w_kv_shared, b_w_in, b_w_out, w_mem_kv, ln1_g, ln1_b, ln2_g, ln2_b, router_w,
                        router_b, exp_w_gate_up, exp_b_gate_up, exp_w_down, exp_b_down)
    bp, lp, d = x_prompt.shape
    bs, ls, _ = x_sample.shape

    mem2 = mem_prompt.reshape(bp * MEM_TOKENS, d)
    p_mem = jnp.stack([_linear([(mem2, d, 0)], [wts["mem_kv"][l]], bm=MEM_TOKENS, bn=512)
                       for l in range(DEPTH)]).reshape(DEPTH, bp, MEM_TOKENS, 2 * MEM_DIM)
    conv0 = jnp.zeros((bp, 8, SSM_CONV_DIM), F32)
    ssm0 = jnp.zeros((bp, SSM_HEADS, SSM_HEAD_DIM, SSM_STATE), F32)
    hp, p_ssm, p_conv, p_win = _trunk(x_prompt, p_mem, conv0, ssm0, wts, rel_bias, None, PROMPT_CFG)

    xs = jnp.pad(x_sample, ((0, 0), (0, SAMPLE_ROWS - ls), (0, 0)))
    c0s = jnp.pad(state_conv[0], ((0, 0), (8 - (SSM_CONV - 1), 0), (0, 0)))
    caches = [c.reshape(bs, c.shape[1], 2 * DIL_OUT_DIM) for c in (cache_win0_kv, cache_win1_kv, cache_win2_kv)]
    s_mem = cache_mem_kv.reshape(DEPTH, bs, MEM_TOKENS, 2 * MEM_DIM)
    hs, s_ssm, s_conv, s_win = _trunk(xs, s_mem, c0s, state_ssm[0], wts, rel_bias, caches, SAMPLE_CFG)

    kvshape = (2, DIL_SLOTS, DIL_HEAD_DIM)
    return (hp.reshape(bp, lp, d),
            hs.reshape(bs, SAMPLE_ROWS, d)[:, :ls],
            p_ssm[None], p_conv[None, :, 8 - (SSM_CONV - 1):],
            p_win[0].reshape(bp, -1, *kvshape), p_win[1].reshape(bp, -1, *kvshape),
            p_win[2].reshape(bp, -1, *kvshape),
            p_mem.reshape(DEPTH, bp, MEM_TOKENS, 2, MEM_HEADS, MEM_HEAD_DIM),
            s_ssm[None], s_conv[None, :, 8 - (SSM_CONV - 1):],
            s_win[0].reshape(bs, -1, *kvshape), s_win[1].reshape(bs, -1, *kvshape),
            s_win[2].reshape(bs, -1, *kvshape))
```
